```python
import jax, jax.numpy as jnp
from jax import lax
import numpy as np

D_MODEL = 1024
BATCH = 2
SEQ = 8192
DEPTH = 4
DEC_BATCH = 32
DEC_SEQ = 1
PAST_LEN = 8192
PAGE_SIZE = 128

N_EVEN = (DEPTH + 1) // 2
N_ODD = DEPTH // 2
D_CONV = D_MODEL // 2
CONV_W = 3
H_B = 4
DK_B = (D_MODEL // 2) // H_B
DV_B = (D_MODEL // 2) // H_B
D_HG = H_B * DK_B
CHUNK_B = 64
PROJ_EVEN = 3 * D_CONV + 4 * D_HG
H_C = 8
DH_C = D_MODEL // H_C
MOBA_BLOCK = 256
MOBA_TOPK = 3
Q_BLOCK = 32
ROPE_THETA = 10000.0
N_MEM = 256
X_HEADS = 4
X_DH = D_MODEL // X_HEADS
D_FF = 4 * D_MODEL
EPS = 1e-6

kernel_name = 'hybrid_conv_hgrn2_moba_decoder_step'

F32 = jnp.float32


def rmsnorm(x, g):
    xf = x.astype(F32)
    r = lax.rsqrt(jnp.mean(xf * xf, axis=-1, keepdims=True) + EPS)
    return (xf * r * g.astype(F32)).astype(x.dtype)


def rope(x, pos):
    half = x.shape[-1] // 2
    inv = ROPE_THETA ** (-jnp.arange(half, dtype=F32) / half)
    ang = pos.astype(F32)[:, None] * inv[None, :]
    cos = jnp.cos(ang)[None, :, None, :]
    sin = jnp.sin(ang)[None, :, None, :]
    xf = x.astype(F32)
    x1, x2 = xf[..., :half], xf[..., half:]
    return jnp.concatenate([x1 * cos - x2 * sin, x2 * cos + x1 * sin], axis=-1).astype(x.dtype)


def hgrn2_chunked(q, k, v, g, s0):
    B, T, H, DK = q.shape
    DV = v.shape[-1]
    L = min(CHUNK_B, T)
    nC = -(-T // L)
    Tp = nC * L

    def prep(a):
        a = jnp.pad(a.astype(F32), ((0, 0), (0, Tp - T), (0, 0), (0, 0)))
        return a.reshape(B, nC, L, H, a.shape[-1]).transpose(1, 0, 3, 2, 4)

    qc, kc, vc, gc = prep(q), prep(k), prep(v), prep(g)
    tril = jnp.tril(jnp.ones((L, L), bool))[None, None, :, :, None]

    def step(S, inp):
        qi, ki, vi, gi = inp
        b = jnp.cumsum(gi, axis=2)
        decay = jnp.exp(jnp.where(tril, b[:, :, :, None, :] - b[:, :, None, :, :], -jnp.inf))
        A = jnp.einsum('bhtd,bhsd,bhtsd->bhts', qi, ki, decay)
        o = jnp.einsum('bhts,bhsv->bhtv', A, vi) + jnp.einsum('bhtd,bhdv->bhtv', qi * jnp.exp(b), S)
        bL = b[:, :, -1:, :]
        S = jnp.exp(bL[:, :, 0, :])[..., None] * S + jnp.einsum('bhsd,bhsv->bhdv', ki * jnp.exp(bL - b), vi)
        return S, o

    S, o = lax.scan(step, s0.astype(F32), (qc, kc, vc, gc))
    o = o.transpose(1, 0, 3, 2, 4).reshape(B, Tp, H, DV)[:, :T]
    return o, S


def even_mixer(h, conv_buf, hg_state, w_in, conv_w, lb, out_norm, w_out):
    B, T, _ = h.shape
    proj = h @ w_in
    cuts = [D_CONV, 2 * D_CONV, 3 * D_CONV, 3 * D_CONV + D_HG, 3 * D_CONV + 2 * D_HG, 3 * D_CONV + 3 * D_HG]
    xa, ba, ca, q, fz, iv, og = jnp.split(proj, cuts, axis=-1)
    u = ca * xa
    u_pad = jnp.concatenate([conv_buf.astype(u.dtype), u], axis=1)
    conv = sum(conv_w[j] * u_pad[:, j:j + T] for j in range(CONV_W))
    ya = ba * conv
    new_buf = u_pad[:, -(CONV_W - 1):]
    fzf = fz.astype(F32)
    lbf = lb.astype(F32)
    f = lbf + (1.0 - lbf) * jax.nn.sigmoid(fzf)
    kk = (1.0 - lbf) * jax.nn.sigmoid(-fzf)
    gl = jnp.log(f)
    heads = lambda a: a.reshape(B, T, H_B, -1)
    o, S = hgrn2_chunked(heads(q), heads(kk), heads(iv), heads(gl), hg_state)
    of = o * lax.rsqrt(jnp.mean(o * o, axis=-1, keepdims=True) + EPS) * out_norm.astype(F32)
    of = of * jax.nn.silu(heads(og).astype(F32))
    yb = of.reshape(B, T, D_HG).astype(h.dtype)
    y = jnp.concatenate([ya, yb], axis=-1) @ w_out
    return y, new_buf, S.astype(hg_state.dtype)


def moba_attention(q, k_parts, v_parts, offset):
    B, T, H, D = q.shape
    Lk = offset + T
    NB = -(-Lk // MOBA_BLOCK)
    pad = NB * MOBA_BLOCK - Lk
    zk = jnp.zeros((B, pad, H, D), k_parts[0].dtype)
    kb = jnp.concatenate(list(k_parts) + [zk], axis=1).reshape(B, NB, MOBA_BLOCK, H, D)
    vb = jnp.concatenate(list(v_parts) + [zk.astype(v_parts[0].dtype)], axis=1).reshape(B, NB, MOBA_BLOCK, H, D)
    kmean = jnp.mean(kb.astype(F32), axis=2)
    n_sel = min(MOBA_TOPK, NB)
    qb = min(Q_BLOCK, T)
    nQ = -(-T // qb)
    Tp = nQ * qb
    qp = jnp.pad(q, ((0, 0), (0, Tp - T), (0, 0), (0, 0)))
    qs = qp.reshape(B, nQ, qb, H, D).transpose(1, 0, 2, 3, 4)
    qpos = (offset + jnp.arange(Tp, dtype=jnp.int32)).reshape(nQ, qb)
    scale = D ** -0.5
    bi = jnp.arange(B)[:, None, None, None]
    hi = jnp.arange(H)[None, :, None, None]

    def one(args):
        qblk, p = args
        c = p // MOBA_BLOCK
        gate = jnp.einsum('bqhd,bnhd->bhqn', qblk.astype(F32), kmean)
        gate = jnp.where(jnp.arange(NB)[None, :] < c[:, None], gate, -jnp.inf)
        _, sel = lax.top_k(gate, n_sel)
        own = jnp.broadcast_to(c[None, None, :, None], (B, H, qb, 1)).astype(sel.dtype)
        idx = jnp.concatenate([sel, own], axis=-1)
        kg = kb[bi, idx, :, hi]
        vg = vb[bi, idx, :, hi]
        s = jnp.einsum('bqhd,bhqjsd->bhqjs', qblk, kg, preferred_element_type=F32) * scale
        slot_ok = jnp.concatenate([jnp.arange(n_sel)[None, :] < c[:, None], jnp.ones((qb, 1), bool)], axis=-1)
        key_pos = idx[..., None] * MOBA_BLOCK + jnp.arange(MOBA_BLOCK)
        mask = slot_ok[None, None, :, :, None] & (key_pos <= p[None, None, :, None, None])
        s = jnp.where(mask, s, -jnp.inf)
        pr = jax.nn.softmax(s.reshape(B, H, qb, -1), axis=-1).reshape(s.shape)
        return jnp.einsum('bhqjs,bhqjsd->bqhd', pr.astype(vg.dtype), vg)

    o = lax.map(one, (qs, qpos))
    return o.transpose(1, 0, 2, 3, 4).reshape(B, Tp, H, D)[:, :T]


def odd_mixer(h, pos, offset, past_k, past_v, w_qkv, w_o):
    B, T, _ = h.shape
    qkv = (h @ w_qkv).reshape(B, T, 3, H_C, DH_C)
    q = rope(qkv[:, :, 0], pos)
    k = rope(qkv[:, :, 1], pos)
    v = qkv[:, :, 2]
    k_parts = [k] if past_k is None else [past_k, k]
    v_parts = [v] if past_v is None else [past_v, v]
    o = moba_attention(q, k_parts, v_parts, offset)
    return o.reshape(B, T, H_C * DH_C) @ w_o, k, v


def cross_attn(h, mk, mv, wq, wo):
    B, T, _ = h.shape
    q = (h @ wq).reshape(B, T, X_HEADS, X_DH)
    s = jnp.einsum('bthd,bmhd->bhtm', q, mk, preferred_element_type=F32) * (X_DH ** -0.5)
    p = jax.nn.softmax(s, axis=-1)
    o = jnp.einsum('bhtm,bmhd->bthd', p.astype(mv.dtype), mv)
    return o.reshape(B, T, X_HEADS * X_DH) @ wo


def gather_pages(pool, page_table):
    g = pool[page_table]
    return g.reshape(g.shape[0], g.shape[1] * g.shape[2], g.shape[3], g.shape[4])


def trunk(x, offset, conv_bufs, hg_states, mem_kv, past_kv, prm):
    T = x.shape[1]
    pos = offset + jnp.arange(T, dtype=jnp.int32)
    new_conv, new_hg, new_k, new_v = [], [], [], []
    for l in range(DEPTH):
        j = l // 2
        h = rmsnorm(x, prm['norm_mix'][l])
        if l % 2 == 0:
            y, cb, st = even_mixer(h, conv_bufs[j], hg_states[j], prm['w_in_even'][j], prm['conv_w'][j],
                                   prm['hgrn_lb'][j], prm['hgrn_out_norm'][j], prm['w_out_even'][j])
            new_conv.append(cb)
            new_hg.append(st)
        else:
            pk, pv = past_kv(j) if past_kv is not None else (None, None)
            y, k, v = odd_mixer(h, pos, offset, pk, pv, prm['w_qkv_odd'][j], prm['w_o_odd'][j])
            new_k.append(k)
            new_v.append(v)
        x = x + y
        h = rmsnorm(x, prm['norm_xattn'][l])
        x = x + cross_attn(h, mem_kv[l][0], mem_kv[l][1], prm['w_xq'][l], prm['w_xo'][l])
        h = rmsnorm(x, prm['norm_mlp'][l])
        x = x + jnp.square(jax.nn.relu(h @ prm['w_up'][l])) @ prm['w_down'][l]
    y = rmsnorm(x, prm['norm_final'])
    return y, jnp.stack(new_conv), jnp.stack(new_hg), jnp.stack(new_k), jnp.stack(new_v)


def setup_inputs(seed: int = 0) -> dict:
    key = jax.random.key(seed)
    ks = jax.random.split(key, 32)
    n_pages = PAST_LEN // PAGE_SIZE
    n_used = DEC_BATCH * n_pages
    n_pool = (n_used * 5) // 4
    nrm = lambda k, shape, s=1.0: jax.random.normal(k, shape, F32) * s
    gain = lambda k, shape: 1.0 + 0.02 * jax.random.normal(k, shape, F32)
    page_table = jax.random.permutation(ks[0], n_pool)[:n_used].reshape(DEC_BATCH, n_pages).astype(jnp.int32)
    return {
        'x_prompt': nrm(ks[1], (BATCH, SEQ, D_MODEL)),
        'x_sample': nrm(ks[2], (DEC_BATCH, DEC_SEQ, D_MODEL)),
        'cache_conv': nrm(ks[3], (N_EVEN, DEC_BATCH, CONV_W - 1, D_CONV)),
        'state_hgrn': nrm(ks[4], (N_EVEN, DEC_BATCH, H_B, DK_B, DV_B), 0.5),
        'cache_k': nrm(ks[5], (N_ODD, n_pool, PAGE_SIZE, H_C, DH_C)),
        'cache_v': nrm(ks[6], (N_ODD, n_pool, PAGE_SIZE, H_C, DH_C)),
        'cache_mem_k': nrm(ks[7], (DEPTH, DEC_BATCH, N_MEM, X_HEADS, X_DH)),
        'cache_mem_v': nrm(ks[8], (DEPTH, DEC_BATCH, N_MEM, X_HEADS, X_DH)),
        'page_table': page_table,
        'mem_prompt': nrm(ks[9], (BATCH, N_MEM, D_MODEL)),
        'norm_mix': gain(ks[10], (DEPTH, D_MODEL)),
        'norm_xattn': gain(ks[11], (DEPTH, D_MODEL)),
        'norm_mlp': gain(ks[12], (DEPTH, D_MODEL)),
        'norm_final': gain(ks[13], (D_MODEL,)),
        'w_in_even': nrm(ks[14], (N_EVEN, D_MODEL, PROJ_EVEN), D_MODEL ** -0.5),
        'conv_w': nrm(ks[15], (N_EVEN, CONV_W, D_CONV), CONV_W ** -0.5),
        'hgrn_lb_logits': nrm(ks[16], (N_EVEN, D_HG), 0.5),
        'hgrn_out_norm': gain(ks[17], (N_EVEN, DV_B)),
        'w_out_even': nrm(ks[18], (N_EVEN, D_CONV + D_HG, D_MODEL), (D_CONV + D_HG) ** -0.5),
        'w_qkv_odd': nrm(ks[19], (N_ODD, D_MODEL, 3 * H_C * DH_C), D_MODEL ** -0.5),
        'w_o_odd': nrm(ks[20], (N_ODD, H_C * DH_C, D_MODEL), (H_C * DH_C) ** -0.5),
        'w_xq': nrm(ks[21], (DEPTH, D_MODEL, X_HEADS * X_DH), D_MODEL ** -0.5),
        'w_xk': nrm(ks[22], (DEPTH, D_MODEL, X_HEADS * X_DH), D_MODEL ** -0.5),
        'w_xv': nrm(ks[23], (DEPTH, D_MODEL, X_HEADS * X_DH), D_MODEL ** -0.5),
        'w_xo': nrm(ks[24], (DEPTH, X_HEADS * X_DH, D_MODEL), (X_HEADS * X_DH) ** -0.5),
        'w_up': nrm(ks[25], (DEPTH, D_MODEL, D_FF), D_MODEL ** -0.5),
        'w_down': nrm(ks[26], (DEPTH, D_FF, D_MODEL), D_FF ** -0.5),
    }


def reference(x_prompt, x_sample, cache_conv, state_hgrn, cache_k, cache_v, cache_mem_k, cache_mem_v,
              page_table, mem_prompt, norm_mix, norm_xattn, norm_mlp, norm_final, w_in_even, conv_w,
              hgrn_lb_logits, hgrn_out_norm, w_out_even, w_qkv_odd, w_o_odd, w_xq, w_xk, w_xv, w_xo,
              w_up, w_down):
    p_lb = jax.nn.softmax(hgrn_lb_logits.astype(F32), axis=0)
    hgrn_lb = jnp.cumsum(p_lb, axis=0) - p_lb[0]
    prm = {'norm_mix': norm_mix, 'norm_xattn': norm_xattn, 'norm_mlp': norm_mlp, 'norm_final': norm_final,
           'w_in_even': w_in_even, 'conv_w': conv_w, 'hgrn_lb': hgrn_lb, 'hgrn_out_norm': hgrn_out_norm,
           'w_out_even': w_out_even, 'w_qkv_odd': w_qkv_odd, 'w_o_odd': w_o_odd,
           'w_xq': w_xq, 'w_xo': w_xo, 'w_up': w_up, 'w_down': w_down}

    Bp = x_prompt.shape[0]
    p_mem_k = jnp.stack([(mem_prompt @ w_xk[l]).reshape(Bp, N_MEM, X_HEADS, X_DH) for l in range(DEPTH)])
    p_mem_v = jnp.stack([(mem_prompt @ w_xv[l]).reshape(Bp, N_MEM, X_HEADS, X_DH) for l in range(DEPTH)])
    conv0 = jnp.zeros((N_EVEN, Bp, CONV_W - 1, D_CONV), x_prompt.dtype)
    hg0 = jnp.zeros((N_EVEN, Bp, H_B, DK_B, DV_B), F32)
    mem_p = [(p_mem_k[l], p_mem_v[l]) for l in range(DEPTH)]
    y_prompt, p_conv, p_hgrn, p_k, p_v = trunk(x_prompt, 0, conv0, hg0, mem_p, None, prm)

    def sample_past(j):
        return gather_pages(cache_k[j], page_table), gather_pages(cache_v[j], page_table)

    mem_s = [(cache_mem_k[l], cache_mem_v[l]) for l in range(DEPTH)]
    y_sample, s_conv, s_hgrn, s_k, s_v = trunk(x_sample, PAST_LEN, cache_conv, state_hgrn, mem_s, sample_past, prm)

    return (y_prompt, y_sample, p_conv, p_hgrn, p_k, p_v, p_mem_k, p_mem_v, s_conv, s_hgrn, s_k, s_v)
```

```python
import functools

import jax
import jax.numpy as jnp
from jax import lax
from jax.experimental import pallas as pl
from jax.experimental.pallas import tpu as pltpu

F32 = jnp.float32
BF16 = jnp.bfloat16

EPS = 1e-6
ROPE_THETA = 10000.0
CONV_W = 3
H_B = 4
H_C = 8
X_HEADS = 4
MOBA_BLOCK = 256
MOBA_TOPK = 3
PAGE_SIZE = 128

LANES = 128
HGRN_CHUNK = 128
HGRN_SUB = 32
EXP_CLAMP = 80.0
NEG = -1e30
VMEM_LIMIT_BYTES = 48 * 1024 * 1024

NT_DIMS = (((1,), (1,)), ((), ()))
TN_DIMS = (((0,), (0,)), ((), ()))


def _params(*sem):
    return pltpu.CompilerParams(dimension_semantics=sem, vmem_limit_bytes=VMEM_LIMIT_BYTES)


def _rms(x, g):
    r = lax.rsqrt(jnp.mean(x * x, axis=-1, keepdims=True) + EPS)
    return x * r * g


def _dot(a, b):
    return jnp.dot(a, b, preferred_element_type=F32)


def _dot_nt(a, b):
    return lax.dot_general(a, b, NT_DIMS, preferred_element_type=F32)


def _dot_tn(a, b):
    return lax.dot_general(a, b, TN_DIMS, preferred_element_type=F32)


def _split_bf16(a):
    hi = a.astype(BF16)
    lo = (a - hi.astype(F32)).astype(BF16)
    return hi, lo


def _dot_nt_x3(a, b):
    ah, al = _split_bf16(a)
    bh, bl = _split_bf16(b)
    return _dot_nt(ah, bh) + (_dot_nt(ah, bl) + _dot_nt(al, bh))


def _sigmoid_pair(z):
    e = jnp.exp(-jnp.abs(z))
    big = 1.0 / (1.0 + e)
    small = e * big
    pos = z >= 0
    return jnp.where(pos, big, small), jnp.where(pos, small, big)


def _norm_matmul_kernel(x_ref, g_ref, w_ref, o_ref, *, norm):
    x = x_ref[...]
    if norm:
        x = _rms(x, g_ref[...])
    o_ref[0] = _dot(x.astype(BF16), w_ref[0]).astype(o_ref.dtype)


def norm_matmul(x, g, w, *, norm=True, tm, out_dtype=F32):
    M, K = x.shape
    G, _, N = w.shape
    assert M % tm == 0
    return pl.pallas_call(
        functools.partial(_norm_matmul_kernel, norm=norm),
        grid=(G, M // tm),
        in_specs=[pl.BlockSpec((tm, K), lambda gi, i: (i, 0)),
                  pl.BlockSpec((1, K), lambda gi, i: (0, 0)),
                  pl.BlockSpec((1, K, N), lambda gi, i: (gi, 0, 0))],
        out_specs=pl.BlockSpec((1, tm, N), lambda gi, i: (gi, i, 0)),
        out_shape=jax.ShapeDtypeStruct((G, M, N), out_dtype),
        compiler_params=_params("arbitrary", "arbitrary"),
        name="norm_matmul",
    )(x, g, w)


def _mm_residual_kernel(a_ref, w_ref, x_ref, o_ref):
    o_ref[...] = x_ref[...] + _dot(a_ref[...].astype(BF16), w_ref[...])


def mm_residual(a, w, x, *, tm):
    M, K = a.shape
    N = w.shape[1]
    assert M % tm == 0
    return pl.pallas_call(
        _mm_residual_kernel,
        grid=(M // tm,),
        in_specs=[pl.BlockSpec((tm, K), lambda i: (i, 0)),
                  pl.BlockSpec((K, N), lambda i: (0, 0)),
                  pl.BlockSpec((tm, N), lambda i: (i, 0))],
        out_specs=pl.BlockSpec((tm, N), lambda i: (i, 0)),
        out_shape=jax.ShapeDtypeStruct((M, N), F32),
        compiler_params=_params("arbitrary"),
        name="mm_residual",
    )(a, w, x)


def _mlp_kernel(x_ref, g_ref, wu_ref, wd_ref, gf_ref, *rest, final_norm):
    if final_norm:
        o_ref, y_ref, h_ref, acc_ref = rest
    else:
        o_ref, h_ref, acc_ref = rest
    f = pl.program_id(1)

    @pl.when(f == 0)
    def _():
        h_ref[...] = _rms(x_ref[...], g_ref[...]).astype(BF16)
        acc_ref[...] = x_ref[...]

    u = _dot(h_ref[...], wu_ref[...])
    u = jnp.square(jnp.maximum(u, 0.0))
    acc_ref[...] += _dot(u.astype(BF16), wd_ref[...])

    @pl.when(f == pl.num_programs(1) - 1)
    def _():
        o_ref[...] = acc_ref[...]
        if final_norm:
            y_ref[...] = _rms(acc_ref[...], gf_ref[...])


def mlp(x, g, w_up, w_down, g_final, *, tm, tf, final_norm):
    M, D = x.shape
    FF = w_up.shape[1]
    assert M % tm == 0 and FF % tf == 0
    row = pl.BlockSpec((tm, D), lambda i, f: (i, 0))
    vec = pl.BlockSpec((1, D), lambda i, f: (0, 0))
    out_shape = [jax.ShapeDtypeStruct((M, D), F32)]
    out_specs = [row]
    if final_norm:
        out_shape.append(jax.ShapeDtypeStruct((M, D), F32))
        out_specs.append(row)
    res = pl.pallas_call(
        functools.partial(_mlp_kernel, final_norm=final_norm),
        grid=(M // tm, FF // tf),
        in_specs=[row, vec,
                  pl.BlockSpec((D, tf), lambda i, f: (0, f)),
                  pl.BlockSpec((tf, D), lambda i, f: (f, 0)),
                  vec],
        out_specs=out_specs,
        out_shape=out_shape,
        scratch_shapes=[pltpu.VMEM((tm, D), BF16), pltpu.VMEM((tm, D), F32)],
        compiler_params=_params("arbitrary", "arbitrary"),
        name="mlp",
    )(x, g, w_up, w_down, g_final)
    return res if final_norm else (res[0], None)


def _xattn_kernel(x_ref, g_ref, wq_ref, mk_ref, mv_ref, wo_ref, o_ref, *, heads):
    x = x_ref[0]
    D = x.shape[-1]
    dh = D // heads
    scale = dh ** -0.5
    q = _dot(_rms(x, g_ref[...]).astype(BF16), wq_ref[...])
    outs = []
    for h in range(heads):
        hs = slice(h * dh, (h + 1) * dh)
        s = _dot_nt(q[:, hs].astype(BF16), mk_ref[0, :, hs].astype(BF16)) * scale
        m = jnp.max(s, axis=-1, keepdims=True)
        p = jnp.exp(s - m)
        l = jnp.sum(p, axis=-1, keepdims=True)
        oh = _dot(p.astype(BF16), mv_ref[0, :, hs].astype(BF16)) / l
        outs.append(oh.astype(BF16))
    o = jnp.concatenate(outs, axis=-1)
    o_ref[0] = x + _dot(o, wo_ref[...])


def xattn_prompt(x, g, wq, mk, mv, wo, *, tm):
    B, T, D = x.shape
    NM = mk.shape[1]
    assert T % tm == 0
    row = pl.BlockSpec((1, tm, D), lambda b, t: (b, t, 0))
    mem = pl.BlockSpec((1, NM, D), lambda b, t: (b, 0, 0))
    wsp = pl.BlockSpec((D, D), lambda b, t: (0, 0))
    return pl.pallas_call(
        functools.partial(_xattn_kernel, heads=X_HEADS),
        grid=(B, T // tm),
        in_specs=[row, pl.BlockSpec((1, D), lambda b, t: (0, 0)), wsp, mem, mem, wsp],
        out_specs=row,
        out_shape=jax.ShapeDtypeStruct((B, T, D), F32),
        compiler_params=_params("arbitrary", "arbitrary"),
        name="xattn_prompt",
    )(x, g, wq, mk, mv, wo)


def _xattn_sample_kernel(q_ref, mk_ref, mv_ref, o_ref, *, heads):
    D = q_ref.shape[-1]
    dh = D // heads
    scale = dh ** -0.5
    for h in range(heads):
        hs = slice(h * dh, (h + 1) * dh)
        q = jnp.broadcast_to(q_ref[0, :, hs], (8, dh)).astype(BF16)
        s = _dot_nt(q, mk_ref[0, :, hs].astype(BF16)) * scale
        m = jnp.max(s, axis=-1, keepdims=True)
        p = jnp.exp(s - m)
        l = jnp.sum(p, axis=-1, keepdims=True)
        oh = _dot(p.astype(BF16), mv_ref[0, :, hs].astype(BF16)) / l
        o_ref[0, :, hs] = oh[0:1].astype(o_ref.dtype)


def xattn_sample_core(q, mk, mv):
    B, _, D = q.shape
    NM = mk.shape[1]
    row = pl.BlockSpec((1, 1, D), lambda b: (b, 0, 0))
    mem = pl.BlockSpec((1, NM, D), lambda b: (b, 0, 0))
    return pl.pallas_call(
        functools.partial(_xattn_sample_kernel, heads=X_HEADS),
        grid=(B,),
        in_specs=[row, mem, mem],
        out_specs=row,
        out_shape=jax.ShapeDtypeStruct((B, 1, D), BF16),
        compiler_params=_params("arbitrary"),
        name="xattn_sample_core",
    )(q, mk, mv)


def _even_core_kernel(proj_ref, cw_ref, lb_ref, on_ref, cb0_ref, s0_ref,
                      y_ref, cb_ref, s_ref, carry_ref, st_ref, *, tm, dc, dh):
    t = pl.program_id(1)
    nh = H_B
    dhg = nh * dh

    @pl.when(t == 0)
    def _():
        carry_ref[...] = jnp.zeros_like(carry_ref)
        carry_ref[6:8, :] = cb0_ref[0]
        for h in range(nh):
            st_ref[h] = s0_ref[0, h].T

    xa = proj_ref[0, :, 0:dc]
    ba = proj_ref[0, :, dc:2 * dc]
    ca = proj_ref[0, :, 2 * dc:3 * dc]
    u = ca * xa
    row = lax.broadcasted_iota(jnp.int32, (tm, dc), 0)
    p1 = carry_ref[7:8, :]
    p2 = carry_ref[6:7, :]
    u1 = jnp.where(row == 0, p1, pltpu.roll(u, 1, axis=0))
    u2 = jnp.where(row == 0, p2, jnp.where(row == 1, p1, pltpu.roll(u, 2, axis=0)))
    conv = cw_ref[0:1, :] * u2 + cw_ref[1:2, :] * u1 + cw_ref[2:3, :] * u
    y_ref[0, :, 0:dc] = (ba * conv).astype(y_ref.dtype)
    carry_ref[...] = u[tm - 8:tm, :]
    cb_ref[0] = u[tm - 2:tm, :]

    base = 3 * dc
    L = HGRN_CHUNK
    nsub = L // HGRN_SUB
    lb = lb_ref[...]
    onorm = on_ref[...]
    rowc = lax.broadcasted_iota(jnp.int32, (L, dhg), 0)
    r2 = lax.broadcasted_iota(jnp.int32, (L, L), 0)
    c2 = lax.broadcasted_iota(jnp.int32, (L, L), 1)
    rsub = r2 // HGRN_SUB

    def chunk(c, _):
        rs = pl.ds(pl.multiple_of(c * L, L), L)
        q = proj_ref[0, rs, base:base + dhg]
        fz = proj_ref[0, rs, base + dhg:base + 2 * dhg]
        iv = proj_ref[0, rs, base + 2 * dhg:base + 3 * dhg]
        og = proj_ref[0, rs, base + 3 * dhg:base + 4 * dhg]
        sig, nsig = _sigmoid_pair(fz)
        f = lb + (1.0 - lb) * sig
        kk = (1.0 - lb) * nsig
        g = jnp.log(f)
        b = g
        sh = 1
        while sh < L:
            b = b + jnp.where(rowc >= sh, pltpu.roll(b, sh, axis=0), 0.0)
            sh *= 2
        bex = b - g
        refs = [bex[i * HGRN_SUB:i * HGRN_SUB + 1, :] for i in range(nsub)]
        rrow = jnp.concatenate([jnp.broadcast_to(r, (HGRN_SUB, dhg)) for r in refs], axis=0)
        qhat = (q * jnp.exp(b - rrow)).astype(BF16)
        khats = [(kk * jnp.exp(jnp.minimum(r - b, EXP_CLAMP))).astype(BF16) for r in refs]
        bl = b[L - 1:L, :]
        qt = (q * jnp.exp(b)).astype(BF16)
        kt = (kk * jnp.exp(bl - b)).astype(BF16)
        dec = jnp.exp(bl)
        vb = iv.astype(BF16)
        gate = og * _sigmoid_pair(og)[0]
        for h in range(nh):
            hs = slice(h * dh, (h + 1) * dh)
            kall = jnp.concatenate([kh[:, hs] for kh in khats], axis=0)
            afull = _dot_nt(qhat[:, hs], kall)
            a = afull[:, 0:L]
            for i in range(1, nsub):
                a = jnp.where(rsub == i, afull[:, i * L:(i + 1) * L], a)
            a = jnp.where(c2 <= r2, a, 0.0)
            st = st_ref[h]
            o = _dot(a.astype(BF16), vb[:, hs]) + _dot_nt(qt[:, hs], st.astype(BF16))
            st_ref[h] = st * dec[:, hs] + _dot_tn(vb[:, hs], kt[:, hs])
            of = o * lax.rsqrt(jnp.mean(o * o, axis=-1, keepdims=True) + EPS) * onorm
            y_ref[0, rs, dc + h * dh:dc + (h + 1) * dh] = (of * gate[:, hs]).astype(y_ref.dtype)
        return 0

    lax.fori_loop(0, tm // L, chunk, 0)

    @pl.when(t == pl.num_programs(1) - 1)
    def _():
        for h in range(nh):
            s_ref[0, h] = st_ref[h].T


def even_core_prompt(proj, conv_w, lb, out_norm, cbuf0, s0, *, tm):
    B, T, P = proj.shape
    dc = conv_w.shape[1]
    dh = out_norm.shape[1]
    dhg = H_B * dh
    assert P == 3 * dc + 4 * dhg and T % tm == 0 and tm % HGRN_CHUNK == 0 and tm >= 8
    return pl.pallas_call(
        functools.partial(_even_core_kernel, tm=tm, dc=dc, dh=dh),
        grid=(B, T // tm),
        in_specs=[pl.BlockSpec((1, tm, P), lambda b, t: (b, t, 0)),
                  pl.BlockSpec((CONV_W, dc), lambda b, t: (0, 0)),
                  pl.BlockSpec((1, dhg), lambda b, t: (0, 0)),
                  pl.BlockSpec((1, dh), lambda b, t: (0, 0)),
                  pl.BlockSpec((1, CONV_W - 1, dc), lambda b, t: (b, 0, 0)),
                  pl.BlockSpec((1, H_B, dh, dh), lambda b, t: (b, 0, 0, 0))],
        out_specs=[pl.BlockSpec((1, tm, dc + dhg), lambda b, t: (b, t, 0)),
                   pl.BlockSpec((1, CONV_W - 1, dc), lambda b, t: (b, 0, 0)),
                   pl.BlockSpec((1, H_B, dh, dh), lambda b, t: (b, 0, 0, 0))],
        out_shape=[jax.ShapeDtypeStruct((B, T, dc + dhg), BF16),
                   jax.ShapeDtypeStruct((B, CONV_W - 1, dc), F32),
                   jax.ShapeDtypeStruct((B, H_B, dh, dh), F32)],
        scratch_shapes=[pltpu.VMEM((8, dc), F32), pltpu.VMEM((H_B, dh, dh), F32)],
        compiler_params=_params("arbitrary", "arbitrary"),
        name="even_core_prompt",
    )(proj, conv_w, lb, out_norm, cbuf0, s0)


def _even_step_kernel(proj_ref, cw_ref, lb_ref, on_ref, cb0_ref, s0_ref, y_ref, cb_ref, s_ref, *, dc, dh):
    nh = H_B
    dhg = nh * dh
    base = 3 * dc
    xa = proj_ref[0, :, 0:dc]
    ba = proj_ref[0, :, dc:2 * dc]
    ca = proj_ref[0, :, 2 * dc:3 * dc]
    u = ca * xa
    conv = cw_ref[0:1, :] * cb0_ref[0, 0:1, :] + cw_ref[1:2, :] * cb0_ref[0, 1:2, :] + cw_ref[2:3, :] * u
    y_ref[0, :, 0:dc] = (ba * conv).astype(y_ref.dtype)
    cb_ref[0, 0:1, :] = cb0_ref[0, 1:2, :]
    cb_ref[0, 1:2, :] = u

    q = proj_ref[0, :, base:base + dhg]
    fz = proj_ref[0, :, base + dhg:base + 2 * dhg]
    iv = proj_ref[0, :, base + 2 * dhg:base + 3 * dhg]
    og = proj_ref[0, :, base + 3 * dhg:base + 4 * dhg]
    lb = lb_ref[...]
    sig, nsig = _sigmoid_pair(fz)
    f = lb + (1.0 - lb) * sig
    kk = (1.0 - lb) * nsig
    gate = og * _sigmoid_pair(og)[0]

    def col(v):
        return jnp.broadcast_to(v, (dh, dh)).T

    for h in range(nh):
        hs = slice(h * dh, (h + 1) * dh)
        s_new = col(f[:, hs]) * s0_ref[0, h] + col(kk[:, hs]) * iv[:, hs]
        s_ref[0, h] = s_new
        o = jnp.sum(col(q[:, hs]) * s_new, axis=0, keepdims=True)
        of = o * lax.rsqrt(jnp.mean(o * o, axis=-1, keepdims=True) + EPS) * on_ref[...]
        y_ref[0, :, dc + h * dh:dc + (h + 1) * dh] = (of * gate[:, hs]).astype(y_ref.dtype)


def even_core_sample(proj, conv_w, lb, out_norm, cbuf0, s0):
    B, _, P = proj.shape
    dc = conv_w.shape[1]
    dh = out_norm.shape[1]
    dhg = H_B * dh
    return pl.pallas_call(
        functools.partial(_even_step_kernel, dc=dc, dh=dh),
        grid=(B,),
        in_specs=[pl.BlockSpec((1, 1, P), lambda b: (b, 0, 0)),
                  pl.BlockSpec((CONV_W, dc), lambda b: (0, 0)),
                  pl.BlockSpec((1, dhg), lambda b: (0, 0)),
                  pl.BlockSpec((1, dh), lambda b: (0, 0)),
                  pl.BlockSpec((1, CONV_W - 1, dc), lambda b: (b, 0, 0)),
                  pl.BlockSpec((1, H_B, dh, dh), lambda b: (b, 0, 0, 0))],
        out_specs=[pl.BlockSpec((1, 1, dc + dhg), lambda b: (b, 0, 0)),
                   pl.BlockSpec((1, CONV_W - 1, dc), lambda b: (b, 0, 0)),
                   pl.BlockSpec((1, H_B, dh, dh), lambda b: (b, 0, 0, 0))],
        out_shape=[jax.ShapeDtypeStruct((B, 1, dc + dhg), BF16),
                   jax.ShapeDtypeStruct((B, CONV_W - 1, dc), F32),
                   jax.ShapeDtypeStruct((B, H_B, dh, dh), F32)],
        compiler_params=_params("arbitrary"),
        name="even_core_sample",
    )(proj, conv_w, lb, out_norm, cbuf0, s0)


def _rope(x, cos2, sin2):
    return x * cos2 + pltpu.roll(x, x.shape[-1] // 2, axis=1) * sin2


def _select_topk(gt, nvalid, nblk):
    n_io = lax.broadcasted_iota(jnp.int32, gt.shape, 0)
    gt = jnp.where(n_io < nvalid, gt, -jnp.inf)
    sel = jnp.zeros(gt.shape, F32)
    picks = []
    for _ in range(MOBA_TOPK):
        m = jnp.max(gt, axis=0, keepdims=True)
        idx = jnp.min(jnp.where(gt == m, n_io, nblk), axis=0, keepdims=True)
        idx = jnp.where(m > -jnp.inf, idx, -1)
        hit = n_io == idx
        sel = jnp.where(hit, 1.0, sel)
        gt = jnp.where(hit, -jnp.inf, gt)
        picks.append(idx)
    return sel, picks


def _odd_prep_kernel(qkv_ref, cos_ref, sin_ref, k32_ref, v32_ref, qt_ref, kb_ref, vt_ref, pen_ref,
                     km_ref, *, dh, nblk):
    i = pl.program_id(1)
    D = H_C * dh

    @pl.when(i == 0)
    def _():
        km_ref[...] = jnp.zeros_like(km_ref)

    cos2 = cos_ref[...]
    sin2 = sin_ref[...]
    blk_io = lax.broadcasted_iota(jnp.int32, (nblk, dh), 0)
    for h in range(H_C):
        hs = slice(h * dh, (h + 1) * dh)
        qr = _rope(qkv_ref[0, :, h * dh:(h + 1) * dh], cos2, sin2)
        kr = _rope(qkv_ref[0, :, D + h * dh:D + (h + 1) * dh], cos2, sin2)
        v = qkv_ref[0, :, 2 * D + h * dh:2 * D + (h + 1) * dh]
        k32_ref[0, :, hs] = kr
        v32_ref[0, :, hs] = v
        kb_ref[0, h] = kr.astype(BF16)
        qt_ref[0, h] = qr.T.astype(BF16)
        vt_ref[0, h, 0] = v.T.astype(BF16)
        km = km_ref[:, hs]
        gt = _dot_nt_x3(km, qr)
        sel, _ = _select_topk(gt, i, nblk)
        pen_ref[0, h] = jnp.where(sel > 0.5, 0.0, NEG)
        km_ref[:, hs] = jnp.where(blk_io == i, jnp.mean(kr, axis=0, keepdims=True), km)


def odd_prep_prompt(qkv, cos2, sin2):
    B, T, D3 = qkv.shape
    D = D3 // 3
    dh = D // H_C
    blk = MOBA_BLOCK
    assert T % blk == 0 and dh == LANES
    nblk = T // blk
    return pl.pallas_call(
        functools.partial(_odd_prep_kernel, dh=dh, nblk=nblk),
        grid=(B, nblk),
        in_specs=[pl.BlockSpec((1, blk, D3), lambda b, i: (b, i, 0)),
                  pl.BlockSpec((blk, dh), lambda b, i: (i, 0)),
                  pl.BlockSpec((blk, dh), lambda b, i: (i, 0))],
        out_specs=[pl.BlockSpec((1, blk, D), lambda b, i: (b, i, 0)),
                   pl.BlockSpec((1, blk, D), lambda b, i: (b, i, 0)),
                   pl.BlockSpec((1, H_C, dh, blk), lambda b, i: (b, 0, 0, i)),
                   pl.BlockSpec((1, H_C, blk, dh), lambda b, i: (b, 0, i, 0)),
                   pl.BlockSpec((1, H_C, 1, dh, blk), lambda b, i: (b, 0, i, 0, 0)),
                   pl.BlockSpec((1, H_C, nblk, blk), lambda b, i: (b, 0, 0, i))],
        out_shape=[jax.ShapeDtypeStruct((B, T, D), F32),
                   jax.ShapeDtypeStruct((B, T, D), F32),
                   jax.ShapeDtypeStruct((B, H_C, dh, T), BF16),
                   jax.ShapeDtypeStruct((B, H_C, T, dh), BF16),
                   jax.ShapeDtypeStruct((B, H_C, nblk, dh, blk), BF16),
                   jax.ShapeDtypeStruct((B, H_C, nblk, T), F32)],
        scratch_shapes=[pltpu.VMEM((nblk, D), F32)],
        compiler_params=_params("arbitrary", "arbitrary"),
        name="odd_prep_prompt",
    )(qkv, cos2, sin2)


def _moba_attn_kernel(qt_ref, kb_ref, vt_ref, pen_ref, o_ref, *, dh):
    i = pl.program_id(2)
    blk = MOBA_BLOCK
    scale = dh ** -0.5
    qt = qt_ref[0, 0]

    def scores(n):
        kn = kb_ref[0, 0, pl.ds(pl.multiple_of(n * blk, blk), blk), :]
        return _dot(kn, qt) * scale

    kio = lax.broadcasted_iota(jnp.int32, (blk, blk), 0)
    qio = lax.broadcasted_iota(jnp.int32, (blk, blk), 1)
    s = jnp.where(kio <= qio, scores(i), NEG)
    m = jnp.max(s, axis=0, keepdims=True)
    p = jnp.exp(s - m)
    l = jnp.sum(p, axis=0, keepdims=True)
    acc = _dot(vt_ref[0, 0, i], p.astype(BF16))

    def body(n, carry):
        m, l, acc = carry
        s = scores(n) + pen_ref[0, 0, pl.ds(n, 1), :]
        m2 = jnp.maximum(m, jnp.max(s, axis=0, keepdims=True))
        alpha = jnp.exp(m - m2)
        p = jnp.exp(s - m2)
        l = alpha * l + jnp.sum(p, axis=0, keepdims=True)
        acc = alpha * acc + _dot(vt_ref[0, 0, n], p.astype(BF16))
        return m2, l, acc

    m, l, acc = lax.fori_loop(0, i, body, (m, l, acc))
    o_ref[0] = (acc / l).T.astype(o_ref.dtype)


def moba_attn_prompt(qt, kb, vt, pen):
    B, H, dh, T = qt.shape
    blk = MOBA_BLOCK
    nblk = T // blk
    return pl.pallas_call(
        functools.partial(_moba_attn_kernel, dh=dh),
        grid=(B, H, nblk),
        in_specs=[pl.BlockSpec((1, 1, dh, blk), lambda b, h, i: (b, h, 0, i)),
                  pl.BlockSpec((1, 1, T, dh), lambda b, h, i: (b, h, 0, 0)),
                  pl.BlockSpec((1, 1, nblk, dh, blk), lambda b, h, i: (b, h, 0, 0, 0)),
                  pl.BlockSpec((1, 1, nblk, blk), lambda b, h, i: (b, h, 0, i))],
        out_specs=pl.BlockSpec((1, blk, dh), lambda b, h, i: (b, i, h)),
        out_shape=jax.ShapeDtypeStruct((B, T, H * dh), BF16),
        compiler_params=_params("arbitrary", "arbitrary", "arbitrary"),
        name="moba_attn_prompt",
    )(qt, kb, vt, pen)


def _kmean_kernel(pt_ref, *refs, nb_step):
    pages = refs[:2 * nb_step]
    o_ref = refs[2 * nb_step]
    for r in range(nb_step):
        s = jnp.sum(pages[2 * r][0, 0], axis=0, keepdims=True) + jnp.sum(pages[2 * r + 1][0, 0], axis=0, keepdims=True)
        o_ref[0, 0, r:r + 1, :] = s * (1.0 / MOBA_BLOCK)


def sample_kmeans(cache_k, page_table):
    NL, _, page, D = cache_k.shape
    B, NP = page_table.shape
    ppb = MOBA_BLOCK // page
    assert ppb == 2 and NP % ppb == 0
    NB = NP // ppb
    nb_step = 8 if NB % 8 == 0 else NB

    def page_spec(r):
        return pl.BlockSpec((1, 1, page, D), lambda l, b, g, pt: (l, pt[b, g * nb_step * ppb + r], 0, 0))

    return pl.pallas_call(
        functools.partial(_kmean_kernel, nb_step=nb_step),
        grid_spec=pltpu.PrefetchScalarGridSpec(
            num_scalar_prefetch=1,
            grid=(NL, B, NB // nb_step),
            in_specs=[page_spec(r) for r in range(ppb * nb_step)],
            out_specs=pl.BlockSpec((1, 1, nb_step, D), lambda l, b, g, pt: (l, b, g, 0))),
        out_shape=jax.ShapeDtypeStruct((NL, B, NB, D), F32),
        compiler_params=_params("arbitrary", "arbitrary", "arbitrary"),
        name="sample_kmeans",
    )(page_table, *([cache_k] * (ppb * nb_step)))


def _odd_prep_sample_kernel(qkv_ref, cos_ref, sin_ref, km_ref, q_ref, k_ref, v_ref, idx_ref, *, dh, nb):
    D = H_C * dh
    cos2 = cos_ref[...]
    sin2 = sin_ref[...]
    cols = []
    for h in range(H_C):
        hs = slice(h * dh, (h + 1) * dh)
        qr = _rope(qkv_ref[0, :, h * dh:(h + 1) * dh], cos2, sin2)
        kr = _rope(qkv_ref[0, :, D + h * dh:D + (h + 1) * dh], cos2, sin2)
        q_ref[0, :, hs] = qr
        k_ref[0, :, hs] = kr
        v_ref[0, :, hs] = qkv_ref[0, :, 2 * D + h * dh:2 * D + (h + 1) * dh]
        cols.append(jnp.sum(km_ref[0, :, hs] * qr, axis=-1, keepdims=True))
    gt = jnp.concatenate(cols, axis=-1)
    _, picks = _select_topk(gt, nb, nb)
    for r, idx in enumerate(picks):
        idx_ref[0, r:r + 1, :] = idx


def odd_prep_sample(qkv, cos2, sin2, kmean):
    B, _, D3 = qkv.shape
    D = D3 // 3
    dh = D // H_C
    NB = kmean.shape[1]
    assert NB >= MOBA_TOPK
    row = pl.BlockSpec((1, 1, D), lambda b: (b, 0, 0))
    tab = pl.BlockSpec((1, dh), lambda b: (0, 0))
    return pl.pallas_call(
        functools.partial(_odd_prep_sample_kernel, dh=dh, nb=NB),
        grid=(B,),
        in_specs=[pl.BlockSpec((1, 1, D3), lambda b: (b, 0, 0)), tab, tab,
                  pl.BlockSpec((1, NB, D), lambda b: (b, 0, 0))],
        out_specs=[row, row, row, pl.BlockSpec((1, MOBA_TOPK, H_C), lambda b: (b, 0, 0))],
        out_shape=[jax.ShapeDtypeStruct((B, 1, D), F32)] * 3 + [jax.ShapeDtypeStruct((B, MOBA_TOPK, H_C), jnp.int32)],
        compiler_params=_params("arbitrary"),
        name="odd_prep_sample",
    )(qkv, cos2, sin2, kmean)


def _attn_sample_kernel(idx_ref, pt_ref, q_ref, kn_ref, vn_ref, *refs, npg, dh):
    kp = refs[:npg]
    vp = refs[npg:2 * npg]
    o_ref = refs[2 * npg]
    scale = dh ** -0.5
    q = q_ref[0, 0]
    q8 = jnp.broadcast_to(q, (8, dh)).astype(BF16)
    s_self = jnp.sum(q * kn_ref[0, 0], axis=-1, keepdims=True) * scale
    ss = [_dot_nt(q8, kp[r][0, 0].astype(BF16))[0:1] * scale for r in range(npg)]
    m = s_self
    for s in ss:
        m = jnp.maximum(m, jnp.max(s, axis=-1, keepdims=True))
    p_self = jnp.exp(s_self - m)
    l = p_self
    acc = p_self * vn_ref[0, 0]
    for r in range(npg):
        p = jnp.exp(ss[r] - m)
        l = l + jnp.sum(p, axis=-1, keepdims=True)
        p8 = jnp.broadcast_to(p, (8, p.shape[-1])).astype(BF16)
        acc = acc + _dot(p8, vp[r][0, 0].astype(BF16))[0:1]
    o_ref[0, 0] = (acc / l).astype(o_ref.dtype)


def attn_sample(q, kn, vn, idx, page_table, cache_k, cache_v, layer):
    B, H, _, dh = q.shape
    page = cache_k.shape[2]
    ppb = MOBA_BLOCK // page
    npg = MOBA_TOPK * ppb
    NP = page_table.shape[1]

    def page_spec(r):
        j, w = divmod(r, ppb)
        return pl.BlockSpec((1, 1, page, dh),
                            lambda b, h, ix, pt: (layer, pt[b * NP + ix[(b * MOBA_TOPK + j) * H + h] * ppb + w], 0, h))

    vec = pl.BlockSpec((1, 1, 1, dh), lambda b, h, ix, pt: (b, h, 0, 0))
    return pl.pallas_call(
        functools.partial(_attn_sample_kernel, npg=npg, dh=dh),
        grid_spec=pltpu.PrefetchScalarGridSpec(
            num_scalar_prefetch=2,
            grid=(B, H),
            in_specs=[vec, vec, vec] + [page_spec(r) for r in range(npg)] * 2,
            out_specs=vec),
        out_shape=jax.ShapeDtypeStruct((B, H, 1, dh), BF16),
        compiler_params=_params("arbitrary", "arbitrary"),
        name="attn_sample",
    )(idx.reshape(-1), page_table.reshape(-1), q, kn, vn, *([cache_k] * npg), *([cache_v] * npg))


def _rope_tables(pos, dh):
    half = dh // 2
    inv = ROPE_THETA ** (-jnp.arange(half, dtype=F32) / half)
    ang = pos.astype(F32)[:, None] * inv[None, :]
    cos, sin = jnp.cos(ang), jnp.sin(ang)
    return jnp.concatenate([cos, cos], axis=-1), jnp.concatenate([-sin, sin], axis=-1)


def _row_tile(m, want):
    return want if m % want == 0 else m


def kernel(x_prompt, x_sample, cache_conv, state_hgrn, cache_k, cache_v, cache_mem_k, cache_mem_v, page_table, mem_prompt, norm_mix, norm_xattn, norm_mlp, norm_final, w_in_even, conv_w, hgrn_lb_logits, hgrn_out_norm, w_out_even, w_qkv_odd, w_o_odd, w_xq, w_xk, w_xv, w_xo, w_up, w_down):
    B, T, D = x_prompt.shape
    DB = x_sample.shape[0]
    depth = norm_mix.shape[0]
    n_even = w_in_even.shape[0]
    n_odd = w_qkv_odd.shape[0]
    dh_c = D // H_C
    NM = mem_prompt.shape[1]
    past_len = page_table.shape[1] * cache_k.shape[2]
    assert x_sample.shape[1] == 1 and past_len % MOBA_BLOCK == 0

    bf = lambda w: w.astype(BF16)
    w_in, w_out, w_qkv, w_o = bf(w_in_even), bf(w_out_even), bf(w_qkv_odd), bf(w_o_odd)
    wq, wo, wu, wd = bf(w_xq), bf(w_xo), bf(w_up), bf(w_down)
    w_kv = bf(jnp.concatenate([w_xk, w_xv], axis=0))
    vec = lambda g: g.reshape(1, -1)

    p_lb = jax.nn.softmax(hgrn_lb_logits.astype(F32), axis=0)
    hgrn_lb = jnp.cumsum(p_lb, axis=0) - p_lb[0]

    mem_kv = norm_matmul(mem_prompt.reshape(B * NM, D), vec(norm_final), w_kv, norm=False,
                         tm=_row_tile(B * NM, 512))
    p_mem_k = mem_kv[:depth].reshape(depth, B, NM, X_HEADS, D // X_HEADS)
    p_mem_v = mem_kv[depth:].reshape(depth, B, NM, X_HEADS, D // X_HEADS)

    cos_p, sin_p = _rope_tables(jnp.arange(T, dtype=jnp.int32), dh_c)
    cos_s, sin_s = _rope_tables(jnp.full((1,), past_len, jnp.int32), dh_c)

    cache_k4 = cache_k.reshape(n_odd, cache_k.shape[1], cache_k.shape[2], D)
    cache_v4 = cache_v.reshape(n_odd, cache_v.shape[1], cache_v.shape[2], D)
    km_s = sample_kmeans(cache_k4, page_table)

    tm_p = _row_tile(T, 512)
    xp = x_prompt
    xs = x_sample.reshape(DB, D)
    p_conv, p_hgrn, p_k, p_v = [], [], [], []
    s_conv, s_hgrn, s_k, s_v = [], [], [], []
    y_p = y_s = None
    for l in range(depth):
        j = l // 2
        if l % 2 == 0:
            proj = norm_matmul(xp.reshape(B * T, D), vec(norm_mix[l]), w_in[j][None], tm=_row_tile(B * T, 256))[0]
            ycat, cb, st = even_core_prompt(
                proj.reshape(B, T, -1), conv_w[j], vec(hgrn_lb[j]), vec(hgrn_out_norm[j]),
                jnp.zeros((B, CONV_W - 1, conv_w.shape[-1]), F32),
                jnp.zeros((B,) + state_hgrn.shape[2:], F32), tm=tm_p)
            p_conv.append(cb)
            p_hgrn.append(st)
            xp = mm_residual(ycat.reshape(B * T, D), w_out[j], xp.reshape(B * T, D), tm=tm_p).reshape(B, T, D)

            proj = norm_matmul(xs, vec(norm_mix[l]), w_in[j][None], tm=DB)[0]
            ycat, cb, st = even_core_sample(proj.reshape(DB, 1, -1), conv_w[j], vec(hgrn_lb[j]),
                                            vec(hgrn_out_norm[j]), cache_conv[j], state_hgrn[j])
            s_conv.append(cb)
            s_hgrn.append(st)
            xs = mm_residual(ycat.reshape(DB, D), w_out[j], xs, tm=DB)
        else:
            qkv = norm_matmul(xp.reshape(B * T, D), vec(norm_mix[l]), w_qkv[j][None], tm=_row_tile(B * T, 256))[0]
            k32, v32, qt, kb, vt, pen = odd_prep_prompt(qkv.reshape(B, T, 3 * D), cos_p, sin_p)
            p_k.append(k32.reshape(B, T, H_C, dh_c))
            p_v.append(v32.reshape(B, T, H_C, dh_c))
            o = moba_attn_prompt(qt, kb, vt, pen)
            xp = mm_residual(o.reshape(B * T, D), w_o[j], xp.reshape(B * T, D), tm=tm_p).reshape(B, T, D)

            qkv = norm_matmul(xs, vec(norm_mix[l]), w_qkv[j][None], tm=DB)[0]
            q_s, k_s, v_s, idx = odd_prep_sample(qkv.reshape(DB, 1, 3 * D), cos_s, sin_s, km_s[j])
            s_k.append(k_s.reshape(DB, 1, H_C, dh_c))
            s_v.append(v_s.reshape(DB, 1, H_C, dh_c))
            heads = lambda a: a.reshape(DB, H_C, 1, dh_c)
            o = attn_sample(heads(q_s), heads(k_s), heads(v_s), idx, page_table, cache_k4, cache_v4, j)
            xs = mm_residual(o.reshape(DB, D), w_o[j], xs, tm=DB)

        xp = xattn_prompt(xp, vec(norm_xattn[l]), wq[l], mem_kv[l].reshape(B, NM, D),
                          mem_kv[depth + l].reshape(B, NM, D), wo[l], tm=tm_p)
        q_s = norm_matmul(xs, vec(norm_xattn[l]), wq[l][None], tm=DB)[0]
        o = xattn_sample_core(q_s.reshape(DB, 1, D), cache_mem_k[l].reshape(DB, NM, D),
                              cache_mem_v[l].reshape(DB, NM, D))
        xs = mm_residual(o.reshape(DB, D), wo[l], xs, tm=DB)

        last = l == depth - 1
        xp2, y_p = mlp(xp.reshape(B * T, D), vec(norm_mlp[l]), wu[l], wd[l], vec(norm_final),
                       tm=tm_p, tf=1024, final_norm=last)
        xp = xp2.reshape(B, T, D)
        xs, y_s = mlp(xs, vec(norm_mlp[l]), wu[l], wd[l], vec(norm_final), tm=DB, tf=1024, final_norm=last)

    return (y_p.reshape(B, T, D), y_s.reshape(DB, 1, D),
            jnp.stack(p_conv), jnp.stack(p_hgrn), jnp.stack(p_k), jnp.stack(p_v), p_mem_k, p_mem_v,
            jnp.stack(s_conv), jnp.stack(s_hgrn), jnp.stack(s_k), jnp.stack(s_v))
```
